```python
import math
import jax
import jax.numpy as jnp
from jax import lax
import numpy as np

D_MODEL = 4096
BATCH = 4
SEQ = 4096
DEPTH = 1

GRID_W = 64
CTX_LEN = 256

DA_HEADS = 8
DA_HEAD_DIM = 128
DA_WIDTH = DA_HEADS * 2 * DA_HEAD_DIM

MLA_HEADS = 16
Q_LORA = 1024
KV_LORA = 512
QK_NOPE = 128
QK_ROPE = 64
V_DIM = 128
MLA_WIDTH = MLA_HEADS * V_DIM

SPLIT_SIZES = (DA_WIDTH, DA_WIDTH, DA_WIDTH, Q_LORA, KV_LORA + QK_ROPE, D_MODEL, D_MODEL)
IN_COLS = 2048 * 3 + Q_LORA + KV_LORA + QK_ROPE + 2 * D_MODEL

N_GROUPS = 8
EXPERTS_PER_GROUP = 8
N_EXPERTS = N_GROUPS * EXPERTS_PER_GROUP
TOP_K = 2
D_EXPERT = 512
EXPERT_BLOCK = 128

Q_BLOCK = 128
ROPE_BASE = 10000.0
EPS = 1e-6

kernel_name = 'hybrid_diff_mla_hmoe_dit_block'


def _rmsnorm(x, g):
    xf = x.astype(jnp.float32)
    y = xf * lax.rsqrt(jnp.mean(xf * xf, axis=-1, keepdims=True) + EPS)
    return (y * g.astype(jnp.float32)).astype(x.dtype)


def _axial_angles(n, rot_dim):
    quarter = rot_dim // 4
    inv_freq = ROPE_BASE ** (-jnp.arange(quarter, dtype=jnp.float32) / quarter)
    rows = n // GRID_W
    row = jnp.repeat(jnp.arange(rows, dtype=jnp.float32), GRID_W)
    col = jnp.tile(jnp.arange(GRID_W, dtype=jnp.float32), rows)
    return jnp.concatenate([row[:, None] * inv_freq, col[:, None] * inv_freq], axis=-1)


def _rope(x, ang):
    ang = ang.reshape((1, ang.shape[0]) + (1,) * (x.ndim - 3) + (ang.shape[-1],))
    cos, sin = jnp.cos(ang), jnp.sin(ang)
    xf = x.astype(jnp.float32)
    x1, x2 = jnp.split(xf, 2, axis=-1)
    return jnp.concatenate([x1 * cos - x2 * sin, x1 * sin + x2 * cos], axis=-1).astype(x.dtype)


def _sweep_queries(fn, *q_arrays):
    b, n = q_arrays[0].shape[:2]
    nb = n // Q_BLOCK
    blocks = tuple(jnp.swapaxes(q.reshape((b, nb, Q_BLOCK) + q.shape[2:]), 0, 1) for q in q_arrays)
    out = lax.map(lambda qs: fn(*qs), blocks)
    out = jnp.swapaxes(out, 0, 1)
    return out.reshape((b, n) + out.shape[3:])


def _features(proj, g_q, w_uq, g_kv, w_ukv, ang_da, ang_mla):
    b, n, _ = proj.shape
    points, acc = [], 0
    for s in SPLIT_SIZES[:-1]:
        acc += s
        points.append(acc)
    dq, dk, dv, cq, ckv, gate_a, gate_b = jnp.split(proj, points, axis=-1)
    dq = dq.reshape(b, n, 2, DA_HEADS, DA_HEAD_DIM)
    dk = dk.reshape(b, n, 2, DA_HEADS, DA_HEAD_DIM)
    dv = dv.reshape(b, n, DA_HEADS, 2 * DA_HEAD_DIM)
    q = (_rmsnorm(cq, g_q) @ w_uq).reshape(b, n, MLA_HEADS, QK_NOPE + QK_ROPE)
    qn, qr = q[..., :QK_NOPE], q[..., QK_NOPE:]
    kv = (_rmsnorm(ckv[..., :KV_LORA], g_kv) @ w_ukv).reshape(b, n, MLA_HEADS, QK_NOPE + V_DIM)
    kn, mv = kv[..., :QK_NOPE], kv[..., QK_NOPE:]
    kr = ckv[..., KV_LORA:]
    if ang_da is not None:
        dq, dk = _rope(dq, ang_da), _rope(dk, ang_da)
        qr, kr = _rope(qr, ang_mla), _rope(kr, ang_mla)
    return (dq, qn, qr, gate_a, gate_b), (dk, dv, kn, kr, mv)


def _mix(queries, keys, lam, lambda_init, g_subln, w_br_a, w_br_b, w_out):
    dq, qn, qr, gate_a, gate_b = queries
    dk, dv, kn, kr, mv = (k.astype(jnp.float32) for k in keys)
    b, n = dq.shape[:2]
    dt = gate_a.dtype

    def diff_block(q_b):
        s = jnp.einsum('bqmhd,bkmhd->bmhqk', q_b.astype(jnp.float32), dk) * (DA_HEAD_DIM ** -0.5)
        p = jax.nn.softmax(s, axis=-1)
        a = p[:, 0] - lam * p[:, 1]
        return jnp.einsum('bhqk,bkhe->bqhe', a, dv)

    o_a = _sweep_queries(diff_block, dq)
    o_a = _rmsnorm(o_a, g_subln) * (1.0 - lambda_init)

    def mla_block(qn_b, qr_b):
        s = (jnp.einsum('bqhd,bkhd->bhqk', qn_b.astype(jnp.float32), kn)
             + jnp.einsum('bqhr,bkr->bhqk', qr_b.astype(jnp.float32), kr)) * ((QK_NOPE + QK_ROPE) ** -0.5)
        p = jax.nn.softmax(s, axis=-1)
        return jnp.einsum('bhqk,bkhd->bqhd', p, mv)

    o_b = _sweep_queries(mla_block, qn, qr)
    y_a = o_a.reshape(b, n, DA_WIDTH).astype(dt) @ w_br_a
    y_b = o_b.reshape(b, n, MLA_WIDTH).astype(dt) @ w_br_b
    merged = jax.nn.sigmoid(gate_a) * y_a + jax.nn.sigmoid(gate_b) * y_b
    return merged @ w_out


def _moe(h, w_grp, b_grp, w_exp, b_exp, w1, w3, w2):
    b, n, d = h.shape
    t = h.reshape(-1, d)
    n_tok = t.shape[0]
    tf = t.astype(jnp.float32)
    grp_p = jax.nn.softmax(tf @ w_grp.astype(jnp.float32) + b_grp.astype(jnp.float32), axis=-1)
    g_idx = jnp.argmax(grp_p, axis=-1)
    g_w = jnp.take_along_axis(grp_p, g_idx[:, None], axis=1)[:, 0]
    e_logits = (tf @ w_exp.astype(jnp.float32) + b_exp.astype(jnp.float32)).reshape(n_tok, N_GROUPS, EXPERTS_PER_GROUP)
    sel = jnp.take_along_axis(e_logits, g_idx[:, None, None], axis=1)[:, 0]
    top_p, top_i = lax.top_k(jax.nn.softmax(sel, axis=-1), TOP_K)
    weights = g_w[:, None] * top_p / jnp.sum(top_p, axis=-1, keepdims=True)
    expert_ids = g_idx[:, None] * EXPERTS_PER_GROUP + top_i

    n_asg = n_tok * TOP_K
    eid = expert_ids.reshape(-1).astype(jnp.int32)
    wts = weights.reshape(-1)
    tok = jnp.repeat(jnp.arange(n_tok, dtype=jnp.int32), TOP_K)
    order = jnp.argsort(eid)
    e_s, tok_s, w_s = eid[order], tok[order], wts[order]
    counts = jnp.bincount(eid, length=N_EXPERTS)
    starts = jnp.cumsum(counts) - counts
    pcounts = (counts + EXPERT_BLOCK - 1) // EXPERT_BLOCK * EXPERT_BLOCK
    pends = jnp.cumsum(pcounts)
    pstarts = pends - pcounts
    pos = pstarts[e_s] + jnp.arange(n_asg, dtype=jnp.int32) - starts[e_s]
    nb = -(-n_asg // EXPERT_BLOCK) + N_EXPERTS
    n_pad = nb * EXPERT_BLOCK
    tok_pad = jnp.full((n_pad,), n_tok, jnp.int32).at[pos].set(tok_s)
    w_pad = jnp.zeros((n_pad,), jnp.float32).at[pos].set(w_s)
    blk_e = jnp.minimum(jnp.searchsorted(pends, jnp.arange(nb, dtype=pends.dtype) * EXPERT_BLOCK, side='right'),
                        N_EXPERTS - 1)
    t_pad = jnp.concatenate([t, jnp.zeros((1, d), t.dtype)], axis=0)

    def one_block(args):
        tok_b, w_b, e = args
        xb = t_pad[tok_b]
        hid = jax.nn.silu(xb @ w1[e]) * (xb @ w3[e])
        return (hid @ w2[e]) * w_b[:, None].astype(t.dtype)

    y = lax.map(one_block, (tok_pad.reshape(nb, EXPERT_BLOCK), w_pad.reshape(nb, EXPERT_BLOCK), blk_e))
    out = jnp.zeros((n_tok + 1, d), y.dtype).at[tok_pad].add(y.reshape(n_pad, d))
    return out[:n_tok].reshape(b, n, d)


def setup_inputs(seed: int = 0) -> dict:
    key = jax.random.key(seed)
    ks = jax.random.split(key, 32)
    f32 = jnp.float32
    L, D = DEPTH, D_MODEL

    def nrm(k, shape, scale):
        return jax.random.normal(k, shape, f32) * scale

    def gain(k, shape):
        return 1.0 + 0.05 * jax.random.normal(k, shape, f32)

    return {
        'x': nrm(ks[0], (BATCH, SEQ, D), 1.0),
        'c': nrm(ks[1], (BATCH, D), 1.0),
        'ctx': nrm(ks[2], (BATCH, CTX_LEN, D), 1.0),
        'c_ctx': nrm(ks[3], (D,), 1.0),
        'w_mod': nrm(ks[4], (L, D, 6 * D), 0.5 * D ** -0.5),
        'b_mod': nrm(ks[5], (L, 6 * D), 0.02),
        'g_attn': gain(ks[6], (L, D)),
        'w_in': nrm(ks[7], (L, D, IN_COLS), D ** -0.5),
        'g_q': gain(ks[8], (L, Q_LORA)),
        'w_uq': nrm(ks[9], (L, Q_LORA, MLA_HEADS * (QK_NOPE + QK_ROPE)), Q_LORA ** -0.5),
        'g_kv': gain(ks[10], (L, KV_LORA)),
        'w_ukv': nrm(ks[11], (L, KV_LORA, MLA_HEADS * (QK_NOPE + V_DIM)), KV_LORA ** -0.5),
        'lam_q1': nrm(ks[12], (L, DA_HEAD_DIM), 0.1),
        'lam_k1': nrm(ks[13], (L, DA_HEAD_DIM), 0.1),
        'lam_q2': nrm(ks[14], (L, DA_HEAD_DIM), 0.1),
        'lam_k2': nrm(ks[15], (L, DA_HEAD_DIM), 0.1),
        'g_subln': gain(ks[16], (L, 2 * DA_HEAD_DIM)),
        'w_br_a': nrm(ks[17], (L, DA_WIDTH, D), DA_WIDTH ** -0.5),
        'w_br_b': nrm(ks[18], (L, MLA_WIDTH, D), MLA_WIDTH ** -0.5),
        'w_out': nrm(ks[19], (L, D, D), D ** -0.5),
        'g_ffn': gain(ks[20], (L, D)),
        'w_grp': nrm(ks[21], (L, D, N_GROUPS), D ** -0.5),
        'b_grp': nrm(ks[22], (L, N_GROUPS), 0.01),
        'w_exp': nrm(ks[23], (L, D, N_EXPERTS), D ** -0.5),
        'b_exp': nrm(ks[24], (L, N_EXPERTS), 0.01),
        'w1': nrm(ks[25], (L, N_EXPERTS, D, D_EXPERT), D ** -0.5),
        'w3': nrm(ks[26], (L, N_EXPERTS, D, D_EXPERT), D ** -0.5),
        'w2': nrm(ks[27], (L, N_EXPERTS, D_EXPERT, D), D_EXPERT ** -0.5),
        'g_final': gain(ks[28], (D,)),
    }


def reference(x, c, ctx, c_ctx, w_mod, b_mod, g_attn, w_in, g_q, w_uq, g_kv, w_ukv,
              lam_q1, lam_k1, lam_q2, lam_k2, g_subln, w_br_a, w_br_b, w_out,
              g_ffn, w_grp, b_grp, w_exp, b_exp, w1, w3, w2, g_final):
    n = x.shape[1]
    ang_da = _axial_angles(n, DA_HEAD_DIM)
    ang_mla = _axial_angles(n, QK_ROPE)
    silu_c = jax.nn.silu(c)
    silu_cc = jax.nn.silu(c_ctx)
    for l in range(DEPTH):
        lambda_init = 0.8 - 0.6 * math.exp(-0.3 * l)
        mod_x = (silu_c @ w_mod[l] + b_mod[l])[:, None, :]
        sh1, sc1, gt1, sh2, sc2, gt2 = jnp.split(mod_x, 6, axis=-1)
        mod_c = silu_cc @ w_mod[l] + b_mod[l]
        csh1, csc1, cgt1, csh2, csc2, cgt2 = jnp.split(mod_c, 6, axis=-1)
        lam = (jnp.exp(jnp.sum(lam_q1[l].astype(jnp.float32) * lam_k1[l].astype(jnp.float32)))
               - jnp.exp(jnp.sum(lam_q2[l].astype(jnp.float32) * lam_k2[l].astype(jnp.float32)))
               + lambda_init)

        h_x = _rmsnorm(x, g_attn[l]) * (1 + sc1) + sh1
        h_c = _rmsnorm(ctx, g_attn[l]) * (1 + csc1) + csh1
        q_x, k_x = _features(h_x @ w_in[l], g_q[l], w_uq[l], g_kv[l], w_ukv[l], ang_da, ang_mla)
        q_c, k_c = _features(h_c @ w_in[l], g_q[l], w_uq[l], g_kv[l], w_ukv[l], None, None)
        keys_x = tuple(jnp.concatenate([kc, kx], axis=1) for kc, kx in zip(k_c, k_x))
        mix_args = (lam, lambda_init, g_subln[l], w_br_a[l], w_br_b[l], w_out[l])
        moe_args = (w_grp[l], b_grp[l], w_exp[l], b_exp[l], w1[l], w3[l], w2[l])

        x = x + gt1 * _mix(q_x, keys_x, *mix_args)
        x = x + gt2 * _moe(_rmsnorm(x, g_ffn[l]) * (1 + sc2) + sh2, *moe_args)
        if l < DEPTH - 1:
            ctx = ctx + cgt1 * _mix(q_c, k_c, *mix_args)
            ctx = ctx + cgt2 * _moe(_rmsnorm(ctx, g_ffn[l]) * (1 + csc2) + csh2, *moe_args)
    return _rmsnorm(x, g_final)
```

```python
import functools
import math
from typing import NamedTuple

import jax
import jax.numpy as jnp
from jax import lax
from jax.experimental import pallas as pl
from jax.experimental.pallas import tpu as pltpu

F32 = jnp.float32
BF16 = jnp.bfloat16

EPS = 1e-6
ROPE_BASE = 10000.0
HEAD = 128
ROPE = 64
LANES = 128
MOD_ROWS = 8
EXPERT_ROWS = 128
VMEM_CAP = 60 * 1024 * 1024
HIGHEST = lax.Precision.HIGHEST


class Cfg(NamedTuple):
    batch: int
    seq: int
    ctx: int
    d: int
    grid_w: int
    da_heads: int
    mla_heads: int
    q_lora: int
    kv_lora: int
    n_groups: int
    epg: int
    d_expert: int
    lambda_init: float

    @property
    def t(self):
        return self.batch * self.seq

    @property
    def r(self):
        return self.batch * (self.seq + self.ctx)

    @property
    def daw(self):
        return self.da_heads * 2 * HEAD

    @property
    def kvw(self):
        w = LANES
        while w < self.kv_lora + LANES:
            w *= 2
        return w

    @property
    def off(self):
        dq, dk, dv = 0, self.daw, 2 * self.daw
        cq = 3 * self.daw
        ckv = cq + self.q_lora
        ga = ckv + self.kvw
        gb = ga + self.d
        return dict(dq=dq, dk=dk, dv=dv, cq=cq, ckv=ckv, ga=ga, gb=gb, end=gb + self.d)

    @property
    def n_experts(self):
        return self.n_groups * self.epg


def _tile(n, pref, mult):
    best = None
    for t in range(mult, min(n, pref) + 1, mult):
        if n % t == 0:
            best = t
    assert best is not None, (n, pref, mult)
    return best


def _params(semantics, vmem_bytes):
    limit = min(int(vmem_bytes * 1.25) + (6 << 20), VMEM_CAP)
    return pltpu.CompilerParams(dimension_semantics=semantics, vmem_limit_bytes=limit)


def _rms(xf, g):
    ms = jnp.mean(xf * xf, axis=-1, keepdims=True)
    return xf * lax.rsqrt(ms + EPS) * g


def _mod_kernel(cc_ref, w_ref, b_ref, lamv_ref, mod_ref, lam_ref, *, lambda_init):
    c = cc_ref[...]
    s = (c * jax.nn.sigmoid(c)).astype(BF16)
    mod_ref[...] = jnp.dot(s, w_ref[...].astype(BF16), preferred_element_type=F32) + b_ref[...]

    @pl.when(pl.program_id(0) == 0)
    def _():
        lv = lamv_ref[...]
        d1 = jnp.sum(lv[0:1] * lv[1:2], axis=-1, keepdims=True)
        d2 = jnp.sum(lv[2:3] * lv[3:4], axis=-1, keepdims=True)
        lam = jnp.exp(d1) - jnp.exp(d2) + lambda_init
        lam_ref[...] = jnp.broadcast_to(lam, lam_ref.shape)


def _modulation(cfg, cc, w_mod, b_mod, lamv):
    d = cfg.d
    n = 6 * d
    tn = _tile(n, 512, LANES)
    return pl.pallas_call(
        functools.partial(_mod_kernel, lambda_init=cfg.lambda_init),
        grid=(n // tn,),
        in_specs=[
            pl.BlockSpec((MOD_ROWS, d), lambda j: (0, 0)),
            pl.BlockSpec((d, tn), lambda j: (0, j)),
            pl.BlockSpec((1, tn), lambda j: (0, j)),
            pl.BlockSpec((4, HEAD), lambda j: (0, 0)),
        ],
        out_specs=[
            pl.BlockSpec((MOD_ROWS, tn), lambda j: (0, j)),
            pl.BlockSpec((MOD_ROWS, LANES), lambda j: (0, 0)),
        ],
        out_shape=[jax.ShapeDtypeStruct((MOD_ROWS, n), F32),
                   jax.ShapeDtypeStruct((MOD_ROWS, LANES), F32)],
        compiler_params=_params(("arbitrary",), 2 * d * tn * 4 + d * tn * 2),
        name="modulation",
    )(cc, w_mod, b_mod, lamv)


def _norm_mod_kernel(x_ref, c_ref, g_ref, mod_ref, h_ref, *, nx, tiles_per_batch, ctx_row, d):
    t = pl.program_id(0)

    def emit(src_ref, row):
        y = _rms(src_ref[...], g_ref[...])
        sh = mod_ref[pl.ds(row, 1), 0:d]
        sc = mod_ref[pl.ds(row, 1), d:2 * d]
        h_ref[...] = (y * (1.0 + sc) + sh).astype(h_ref.dtype)

    @pl.when(t < nx)
    def _():
        emit(x_ref, t // tiles_per_batch)

    @pl.when(t >= nx)
    def _():
        emit(c_ref, ctx_row)


def _norm_modulate(cfg, x2, ctx2, g, mod):
    d = cfg.d
    tr = _tile(math.gcd(cfg.seq, cfg.ctx), 256, 8)
    nx, nc = cfg.t // tr, (cfg.r - cfg.t) // tr
    return pl.pallas_call(
        functools.partial(_norm_mod_kernel, nx=nx, tiles_per_batch=cfg.seq // tr, ctx_row=cfg.batch, d=d),
        grid=(nx + nc,),
        in_specs=[
            pl.BlockSpec((tr, d), lambda t: (jnp.minimum(t, nx - 1), 0)),
            pl.BlockSpec((tr, d), lambda t: (jnp.maximum(t - nx, 0), 0)),
            pl.BlockSpec((1, d), lambda t: (0, 0)),
            pl.BlockSpec((MOD_ROWS, 6 * d), lambda t: (0, 0)),
        ],
        out_specs=pl.BlockSpec((tr, d), lambda t: (t, 0)),
        out_shape=jax.ShapeDtypeStruct((cfg.r, d), BF16),
        compiler_params=_params(("arbitrary",), 2 * (2 * tr * d * 4 + tr * d * 2 + MOD_ROWS * 6 * d * 4)),
        name="norm_modulate",
    )(x2, ctx2, g, mod)


def _matmul_kernel(a_ref, b_ref, o_ref):
    o_ref[...] = jnp.dot(a_ref[...], b_ref[...], preferred_element_type=F32).astype(o_ref.dtype)


def _matmul(a, b, out_dtype, name):
    m, k = a.shape
    _, n = b.shape
    tm = _tile(m, 1024, 256)
    tn = _tile(n, 1024, LANES)
    return pl.pallas_call(
        _matmul_kernel,
        grid=(m // tm, n // tn),
        in_specs=[pl.BlockSpec((tm, k), lambda i, j: (i, 0)),
                  pl.BlockSpec((k, tn), lambda i, j: (0, j))],
        out_specs=pl.BlockSpec((tm, tn), lambda i, j: (i, j)),
        out_shape=jax.ShapeDtypeStruct((m, n), out_dtype),
        compiler_params=_params(("arbitrary", "arbitrary"),
                                2 * (tm * k * 2 + k * tn * 2 + tm * tn * 2) + tm * tn * 4),
        name=name,
    )(a, b)


def _rope_half(x, cos, sin):
    return x * cos + pltpu.roll(x, 64, 1) * sin


def _rope_quarter(x, cos, sin):
    return x * cos + (pltpu.roll(x, 96, 1) + pltpu.roll(x, 32, 1)) * sin


def _feature_kernel(dq_ref, dk_ref, cq_ref, ckv_ref, tab_ref, gq_ref, gkv_ref, wuq_ref, wukv_ref,
                    qd_ref, kd_ref, qm_ref, km_ref, vm_ref, *, n_da, n_mla, kv_lora, q_scale):
    for j in range(n_da):
        sl = slice(j * HEAD, (j + 1) * HEAD)
        qd_ref[:, sl] = _rope_half(dq_ref[:, sl].astype(F32), tab_ref[0], tab_ref[1]).astype(BF16)
        kd_ref[:, sl] = _rope_half(dk_ref[:, sl].astype(F32), tab_ref[2], tab_ref[3]).astype(BF16)

    cqn = _rms(cq_ref[...].astype(F32), gq_ref[...]).astype(BF16)
    q_all = jnp.dot(cqn, wuq_ref[...], preferred_element_type=F32)
    for h in range(n_mla):
        lo, mid, hi = 2 * h * HEAD, (2 * h + 1) * HEAD, (2 * h + 2) * HEAD
        qm_ref[:, lo:mid] = (q_all[:, lo:mid] * q_scale).astype(BF16)
        qm_ref[:, mid:hi] = _rope_quarter(q_all[:, mid:hi], tab_ref[4], tab_ref[5]).astype(BF16)

    ckv = ckv_ref[...].astype(F32)
    cn = _rms(ckv[:, :kv_lora], gkv_ref[...]).astype(BF16)
    kv_all = jnp.dot(cn, wukv_ref[...], preferred_element_type=F32)
    kr = _rope_quarter(ckv[:, kv_lora:kv_lora + HEAD], tab_ref[6], tab_ref[7]).astype(BF16)
    for h in range(n_mla):
        lo, mid, hi = 2 * h * HEAD, (2 * h + 1) * HEAD, (2 * h + 2) * HEAD
        km_ref[:, lo:mid] = kv_all[:, lo:mid].astype(BF16)
        km_ref[:, mid:hi] = kr
        vm_ref[:, h * HEAD:(h + 1) * HEAD] = kv_all[:, mid:hi].astype(BF16)


def _features(cfg, proj, tab, g_q, g_kv, wuq_p, wukv):
    off = cfg.off
    r, daw, ql, kvw, hm = cfg.r, cfg.daw, cfg.q_lora, cfg.kvw, cfg.mla_heads
    tr = _tile(math.gcd(cfg.seq, cfg.ctx), 256, 8)
    nx, per_batch = cfg.t // tr, cfg.seq // tr
    assert off["cq"] % ql == 0 and off["ckv"] % kvw == 0

    def tab_map(t):
        return (0, jnp.where(t < nx, t % per_batch, per_batch), 0)

    mw = hm * 2 * HEAD
    vmem = 2 * (tr * (2 * daw + ql + kvw) * 2 + 8 * tr * LANES * 4 + (ql + cfg.kv_lora) * mw * 2
                + tr * (2 * daw + 2 * mw + hm * HEAD) * 2) + 3 * tr * mw * 4
    return pl.pallas_call(
        functools.partial(_feature_kernel, n_da=daw // HEAD, n_mla=hm, kv_lora=cfg.kv_lora,
                          q_scale=float((HEAD + ROPE) ** -0.5)),
        grid=(r // tr,),
        in_specs=[
            pl.BlockSpec((tr, daw), lambda t: (t, off["dq"] // daw)),
            pl.BlockSpec((tr, daw), lambda t: (t, off["dk"] // daw)),
            pl.BlockSpec((tr, ql), lambda t: (t, off["cq"] // ql)),
            pl.BlockSpec((tr, kvw), lambda t: (t, off["ckv"] // kvw)),
            pl.BlockSpec((8, tr, LANES), tab_map),
            pl.BlockSpec((1, ql), lambda t: (0, 0)),
            pl.BlockSpec((1, cfg.kv_lora), lambda t: (0, 0)),
            pl.BlockSpec((ql, mw), lambda t: (0, 0)),
            pl.BlockSpec((cfg.kv_lora, mw), lambda t: (0, 0)),
        ],
        out_specs=[
            pl.BlockSpec((tr, daw), lambda t: (t, 0)),
            pl.BlockSpec((tr, daw), lambda t: (t, 0)),
            pl.BlockSpec((tr, mw), lambda t: (t, 0)),
            pl.BlockSpec((tr, mw), lambda t: (t, 0)),
            pl.BlockSpec((tr, hm * HEAD), lambda t: (t, 0)),
        ],
        out_shape=[
            jax.ShapeDtypeStruct((r, daw), BF16),
            jax.ShapeDtypeStruct((r, daw), BF16),
            jax.ShapeDtypeStruct((r, mw), BF16),
            jax.ShapeDtypeStruct((r, mw), BF16),
            jax.ShapeDtypeStruct((r, hm * HEAD), BF16),
        ],
        compiler_params=_params(("arbitrary",), vmem),
        name="attn_features",
    )(proj, proj, proj, proj, tab, g_q, g_kv, wuq_p, wukv)


def _rope_tables(cfg, tr):
    n = cfg.seq
    pos = jnp.arange(n, dtype=jnp.int32)
    row = (pos // cfg.grid_w).astype(F32)[:, None]
    col = (pos % cfg.grid_w).astype(F32)[:, None]

    def angles(rot_dim):
        quarter = rot_dim // 4
        inv_freq = ROPE_BASE ** (-jnp.arange(quarter, dtype=F32) / quarter)
        return jnp.concatenate([row * inv_freq, col * inv_freq], axis=-1)

    a_da, a_m = angles(HEAD), angles(ROPE)
    zeros = jnp.zeros((n, LANES - ROPE), F32)
    cos_da = jnp.concatenate([jnp.cos(a_da)] * 2, axis=-1)
    sin_da = jnp.concatenate([-jnp.sin(a_da), jnp.sin(a_da)], axis=-1)
    cos_m = jnp.concatenate([jnp.cos(a_m)] * 2 + [zeros], axis=-1)
    sin_m = jnp.concatenate([-jnp.sin(a_m), jnp.sin(a_m), zeros], axis=-1)
    ident_cos = jnp.ones((tr, LANES), F32)
    ident_sin = jnp.zeros((tr, LANES), F32)
    s_da, s_m = HEAD ** -0.5, (HEAD + ROPE) ** -0.5
    tabs = []
    for cos, sin, scale in ((cos_da, sin_da, s_da), (cos_da, sin_da, 1.0), (cos_m, sin_m, s_m), (cos_m, sin_m, 1.0)):
        tabs.append(jnp.concatenate([cos, ident_cos], axis=0) * scale)
        tabs.append(jnp.concatenate([sin, ident_sin], axis=0) * scale)
    return jnp.stack(tabs, axis=0)


def _scores(q, kx, kc):
    dn = (((1,), (1,)), ((), ()))
    sx = lax.dot_general(q, kx, dn, preferred_element_type=F32)
    sc = lax.dot_general(q, kc, dn, preferred_element_type=F32)
    m = jnp.maximum(jnp.max(sx, axis=-1, keepdims=True), jnp.max(sc, axis=-1, keepdims=True))
    px, pc = jnp.exp(sx - m), jnp.exp(sc - m)
    inv = 1.0 / (jnp.sum(px, axis=-1, keepdims=True) + jnp.sum(pc, axis=-1, keepdims=True))
    return px, pc, inv


def _diff_attn_kernel(q0_ref, q1_ref, k0x_ref, k1x_ref, k0c_ref, k1c_ref, vx_ref, vc_ref, lam_ref, g_ref,
                      o_ref, *, out_scale):
    p0x, p0c, r0 = _scores(q0_ref[...], k0x_ref[...], k0c_ref[...])
    p1x, p1c, r1 = _scores(q1_ref[...], k1x_ref[...], k1c_ref[...])
    r1 = r1 * lam_ref[0:1, 0:1]
    ax = (p0x * r0 - p1x * r1).astype(BF16)
    ac = (p0c * r0 - p1c * r1).astype(BF16)
    o = (jnp.dot(ax, vx_ref[...], preferred_element_type=F32)
         + jnp.dot(ac, vc_ref[...], preferred_element_type=F32))
    o_ref[...] = (_rms(o, g_ref[...]) * out_scale).astype(o_ref.dtype)


def _diff_attention(cfg, qd, kd, proj, lam, g_subln, tq_pref=256):
    b, n, c, h = cfg.batch, cfg.seq, cfg.ctx, cfg.da_heads
    tq = _tile(n, tq_pref, 8)
    nq = n // tq
    cblk = cfg.t // c
    vblk = cfg.off["dv"] // (2 * HEAD)
    assert cfg.t % c == 0
    qspec = lambda m: pl.BlockSpec((tq, HEAD), lambda bi, hi, i: (bi * nq + i, m * h + hi))
    kxspec = lambda m: pl.BlockSpec((n, HEAD), lambda bi, hi, i: (bi, m * h + hi))
    kcspec = lambda m: pl.BlockSpec((c, HEAD), lambda bi, hi, i: (cblk + bi, m * h + hi))
    vmem = 2 * (2 * tq * HEAD * 2 + 2 * (n + c) * HEAD * 2 + (n + c) * 2 * HEAD * 2 + tq * 2 * HEAD * 2) \
        + 5 * tq * (n + c) * 4
    return pl.pallas_call(
        functools.partial(_diff_attn_kernel, out_scale=1.0 - cfg.lambda_init),
        grid=(b, h, nq),
        in_specs=[
            qspec(0), qspec(1), kxspec(0), kxspec(1), kcspec(0), kcspec(1),
            pl.BlockSpec((n, 2 * HEAD), lambda bi, hi, i: (bi, vblk + hi)),
            pl.BlockSpec((c, 2 * HEAD), lambda bi, hi, i: (cblk + bi, vblk + hi)),
            pl.BlockSpec((MOD_ROWS, LANES), lambda bi, hi, i: (0, 0)),
            pl.BlockSpec((1, 2 * HEAD), lambda bi, hi, i: (0, 0)),
        ],
        out_specs=pl.BlockSpec((tq, 2 * HEAD), lambda bi, hi, i: (bi * nq + i, hi)),
        out_shape=jax.ShapeDtypeStruct((cfg.t, cfg.daw), BF16),
        compiler_params=_params(("arbitrary", "arbitrary", "arbitrary"), vmem),
        name="diff_attention",
    )(qd, qd, kd, kd, kd, kd, proj, proj, lam, g_subln)


def _mla_attn_kernel(q_ref, kx_ref, kc_ref, vx_ref, vc_ref, o_ref):
    px, pc, inv = _scores(q_ref[...], kx_ref[...], kc_ref[...])
    o = (jnp.dot(px.astype(BF16), vx_ref[...], preferred_element_type=F32)
         + jnp.dot(pc.astype(BF16), vc_ref[...], preferred_element_type=F32))
    o_ref[...] = (o * inv).astype(o_ref.dtype)


def _mla_attention(cfg, qm, km, vm, tq_pref=256):
    b, n, c, h = cfg.batch, cfg.seq, cfg.ctx, cfg.mla_heads
    tq = _tile(n, tq_pref, 8)
    nq = n // tq
    cblk = cfg.t // c
    vmem = 2 * (tq * 2 * HEAD * 2 + (n + c) * 3 * HEAD * 2 + tq * HEAD * 2) + 4 * tq * (n + c) * 4
    return pl.pallas_call(
        _mla_attn_kernel,
        grid=(b, h, nq),
        in_specs=[
            pl.BlockSpec((tq, 2 * HEAD), lambda bi, hi, i: (bi * nq + i, hi)),
            pl.BlockSpec((n, 2 * HEAD), lambda bi, hi, i: (bi, hi)),
            pl.BlockSpec((c, 2 * HEAD), lambda bi, hi, i: (cblk + bi, hi)),
            pl.BlockSpec((n, HEAD), lambda bi, hi, i: (bi, hi)),
            pl.BlockSpec((c, HEAD), lambda bi, hi, i: (cblk + bi, hi)),
        ],
        out_specs=pl.BlockSpec((tq, HEAD), lambda bi, hi, i: (bi * nq + i, hi)),
        out_shape=jax.ShapeDtypeStruct((cfg.t, h * HEAD), BF16),
        compiler_params=_params(("arbitrary", "arbitrary", "arbitrary"), vmem),
        name="mla_attention",
    )(qm, km, km, vm, vm)


def _merge_kernel(oa_ref, ob_ref, wa_ref, wb_ref, ga_ref, gb_ref, o_ref):
    ya = jnp.dot(oa_ref[...], wa_ref[...], preferred_element_type=F32)
    yb = jnp.dot(ob_ref[...], wb_ref[...], preferred_element_type=F32)
    ga = jax.nn.sigmoid(ga_ref[...].astype(F32))
    gb = jax.nn.sigmoid(gb_ref[...].astype(F32))
    o_ref[...] = (ga * ya + gb * yb).astype(o_ref.dtype)


def _merge_branches(cfg, o_a, o_b, w_a, w_b, proj):
    t, d = cfg.t, cfg.d
    ka, kb = o_a.shape[1], o_b.shape[1]
    tm = _tile(t, 1024, 256)
    tn = _tile(d, 512, LANES)
    ga_blk, gb_blk = cfg.off["ga"] // tn, cfg.off["gb"] // tn
    vmem = 2 * (tm * (ka + kb) * 2 + (ka + kb) * tn * 2 + 3 * tm * tn * 2) + 4 * tm * tn * 4
    return pl.pallas_call(
        _merge_kernel,
        grid=(t // tm, d // tn),
        in_specs=[
            pl.BlockSpec((tm, ka), lambda i, j: (i, 0)),
            pl.BlockSpec((tm, kb), lambda i, j: (i, 0)),
            pl.BlockSpec((ka, tn), lambda i, j: (0, j)),
            pl.BlockSpec((kb, tn), lambda i, j: (0, j)),
            pl.BlockSpec((tm, tn), lambda i, j: (i, ga_blk + j)),
            pl.BlockSpec((tm, tn), lambda i, j: (i, gb_blk + j)),
        ],
        out_specs=pl.BlockSpec((tm, tn), lambda i, j: (i, j)),
        out_shape=jax.ShapeDtypeStruct((t, d), BF16),
        compiler_params=_params(("arbitrary", "arbitrary"), vmem),
        name="merge_branches",
    )(o_a, o_b, w_a, w_b, proj, proj)


def _out_proj_kernel(m_ref, w_ref, x_ref, gate_ref, o_ref, *, tiles_per_batch):
    b = pl.program_id(0) // tiles_per_batch
    y = jnp.dot(m_ref[...], w_ref[...], preferred_element_type=F32)
    o_ref[...] = x_ref[...] + gate_ref[pl.ds(b, 1), :] * y


def _out_projection(cfg, merged, w_out, x2, mod):
    t, d = cfg.t, cfg.d
    tm = _tile(cfg.seq, 1024, 256)
    tn = _tile(d, 512, LANES)
    gate_blk = 2 * d // tn
    vmem = 2 * (tm * d * 2 + d * tn * 2 + 2 * tm * tn * 4 + MOD_ROWS * tn * 4) + tm * tn * 4
    return pl.pallas_call(
        functools.partial(_out_proj_kernel, tiles_per_batch=cfg.seq // tm),
        grid=(t // tm, d // tn),
        in_specs=[
            pl.BlockSpec((tm, d), lambda i, j: (i, 0)),
            pl.BlockSpec((d, tn), lambda i, j: (0, j)),
            pl.BlockSpec((tm, tn), lambda i, j: (i, j)),
            pl.BlockSpec((MOD_ROWS, tn), lambda i, j: (0, gate_blk + j)),
        ],
        out_specs=pl.BlockSpec((tm, tn), lambda i, j: (i, j)),
        out_shape=jax.ShapeDtypeStruct((t, d), F32),
        compiler_params=_params(("arbitrary", "arbitrary"), vmem),
        name="out_projection",
    )(merged, w_out, x2, mod)


def _first_lane(mask, lanef):
    return jnp.min(jnp.where(mask, lanef, float(4 * LANES)), axis=-1, keepdims=True)


def _router_kernel(x_ref, g_ref, mod_ref, wr_ref, br_ref, h_ref, route_ref, *, tiles_per_batch, d, n_groups, epg):
    b = pl.program_id(0) // tiles_per_batch
    y = _rms(x_ref[...], g_ref[...])
    sh = mod_ref[pl.ds(b, 1), 3 * d:4 * d]
    sc = mod_ref[pl.ds(b, 1), 4 * d:5 * d]
    h = y * (1.0 + sc) + sh
    h_ref[...] = h

    logits = jnp.dot(h, wr_ref[...], precision=HIGHEST, preferred_element_type=F32) + br_ref[...]
    lane = lax.broadcasted_iota(jnp.int32, logits.shape, 1)
    lanef = lane.astype(F32)
    neg = -1e30
    n_exp = n_groups * epg

    gmask = lane < n_groups
    gl = jnp.where(gmask, logits, neg)
    gm = jnp.max(gl, axis=-1, keepdims=True)
    gz = jnp.sum(jnp.where(gmask, jnp.exp(gl - gm), 0.0), axis=-1, keepdims=True)
    g_w = 1.0 / gz
    g_idx = _first_lane(gmask & (gl == gm), lanef)

    egrp = jnp.right_shift(lane - n_groups, epg.bit_length() - 1).astype(F32)
    emask = (lane >= n_groups) & (lane < n_groups + n_exp) & (egrp == g_idx)
    el = jnp.where(emask, logits, neg)
    em = jnp.max(el, axis=-1, keepdims=True)
    ex = jnp.where(emask, jnp.exp(el - em), 0.0)
    p = ex / jnp.sum(ex, axis=-1, keepdims=True)
    i1 = _first_lane(emask & (el == em), lanef)
    mask2 = emask & (lanef != i1)
    el2 = jnp.where(mask2, logits, neg)
    em2 = jnp.max(el2, axis=-1, keepdims=True)
    i2 = _first_lane(mask2 & (el2 == em2), lanef)
    p1 = jnp.sum(jnp.where(lanef == i1, p, 0.0), axis=-1, keepdims=True)
    p2 = jnp.sum(jnp.where(lanef == i2, p, 0.0), axis=-1, keepdims=True)
    psum = p1 + p2
    w1 = g_w * p1 / psum
    w2 = g_w * p2 / psum
    route_ref[...] = jnp.where(lane == 0, i1 - n_groups,
                     jnp.where(lane == 1, i2 - n_groups,
                     jnp.where(lane == 2, w1,
                     jnp.where(lane == 3, w2, 0.0))))


def _moe_router(cfg, x1, g_ffn, mod, w_r, b_r):
    t, d = cfg.t, cfg.d
    tr = _tile(cfg.seq, 256, 8)
    vmem = 2 * (2 * tr * d * 4 + MOD_ROWS * 6 * d * 4 + d * LANES * 4 + tr * LANES * 4) + 3 * tr * d * 4
    return pl.pallas_call(
        functools.partial(_router_kernel, tiles_per_batch=cfg.seq // tr, d=d, n_groups=cfg.n_groups, epg=cfg.epg),
        grid=(t // tr,),
        in_specs=[
            pl.BlockSpec((tr, d), lambda i: (i, 0)),
            pl.BlockSpec((1, d), lambda i: (0, 0)),
            pl.BlockSpec((MOD_ROWS, 6 * d), lambda i: (0, 0)),
            pl.BlockSpec((d, LANES), lambda i: (0, 0)),
            pl.BlockSpec((1, LANES), lambda i: (0, 0)),
        ],
        out_specs=[pl.BlockSpec((tr, d), lambda i: (i, 0)),
                   pl.BlockSpec((tr, LANES), lambda i: (i, 0))],
        out_shape=[jax.ShapeDtypeStruct((t, d), F32),
                   jax.ShapeDtypeStruct((t, LANES), F32)],
        compiler_params=_params(("arbitrary",), vmem),
        name="moe_router",
    )(x1, g_ffn, mod, w_r, b_r)


def _dispatch_plan(cfg, route):
    t, n_e, bm = cfg.t, cfg.n_experts, EXPERT_ROWS
    n_asg = 2 * t
    nb = n_asg // bm + n_e
    eid = jnp.concatenate([route[:, 0], route[:, 1]]).astype(jnp.int32)
    onehot = (eid[:, None] == jnp.arange(n_e, dtype=jnp.int32)[None, :]).astype(jnp.int32)
    csum = jnp.cumsum(onehot, axis=0)
    counts = csum[-1]
    rank = jnp.sum(onehot * csum, axis=1) - 1
    nbe = (counts + bm - 1) // bm
    bend = jnp.cumsum(nbe)
    bstart = bend - nbe
    pos = jnp.sum(onehot * bstart[None, :], axis=1) * bm + rank
    table = jnp.full((nb * bm,), -1, jnp.int32).at[pos].set(jnp.arange(n_asg, dtype=jnp.int32))
    n_act = bend[-1].astype(jnp.int32)
    blocks = jnp.arange(nb, dtype=jnp.int32)
    blk_e = jnp.minimum(jnp.searchsorted(bend, blocks, side="right"), n_e - 1).astype(jnp.int32)
    blk_e = jnp.where(blocks < n_act, blk_e, blk_e[jnp.maximum(n_act - 1, 0)])
    return blk_e, n_act.reshape(1), table.reshape(nb, bm)


def _expert_kernel(blk_e_ref, n_act_ref, table_hbm, h_hbm, w1_ref, w3_ref, w2_ref, y_hbm,
                   idx_smem, xbuf, ybuf, idx_sem, in_sem, out_sem, *, n_tok, rows):
    del blk_e_ref
    s = pl.program_id(0)
    n_act = n_act_ref[0]

    def idx_copy(block, slot):
        return pltpu.make_async_copy(table_hbm.at[block], idx_smem.at[slot], idx_sem.at[slot])

    def row_in(slot, tok, r):
        return pltpu.make_async_copy(h_hbm.at[pl.ds(tok, 1), :], xbuf.at[slot, pl.ds(r, 1), :], in_sem.at[slot])

    def row_out(dst, r):
        return pltpu.make_async_copy(ybuf.at[pl.ds(r, 1), :], y_hbm.at[pl.ds(dst, 1), :], out_sem.at[0])

    def for_rows(fn):
        def body(r, carry):
            fn(r)
            return carry
        lax.fori_loop(0, rows, body, 0, unroll=8)

    def gather_start(slot, islot):
        def issue(r):
            a = idx_smem[islot, r]
            row_in(slot, jnp.where(a < 0, 0, jnp.where(a >= n_tok, a - n_tok, a)), r).start()
        for_rows(issue)

    def scatter_start(islot):
        def issue(r):
            a = idx_smem[islot, r]
            row_out(jnp.where(a < 0, 2 * n_tok + r, a), r).start()
        for_rows(issue)

    def gather_wait(slot):
        for_rows(lambda r: row_in(slot, 0, r).wait())

    def scatter_wait():
        for_rows(lambda r: row_out(2 * n_tok + r, r).wait())

    @pl.when(s < n_act)
    def _():
        slot, islot = s % 2, s % 3

        @pl.when(s == 0)
        def _():
            ybuf[...] = jnp.zeros_like(ybuf)
            tail = pltpu.make_async_copy(ybuf, y_hbm.at[pl.ds(2 * n_tok, rows), :], out_sem.at[1])
            tail.start()
            tail.wait()
            idx_copy(0, 0).start()
            idx_copy(0, 0).wait()
            gather_start(0, 0)

            @pl.when(n_act > 1)
            def _():
                idx_copy(1, 1).start()

        @pl.when(s + 1 < n_act)
        def _():
            nslot, nislot = (s + 1) % 2, (s + 1) % 3
            idx_copy(s + 1, nislot).wait()
            gather_start(nslot, nislot)

        @pl.when(s + 2 < n_act)
        def _():
            idx_copy(s + 2, (s + 2) % 3).start()

        gather_wait(slot)
        xb = xbuf[slot].astype(BF16)
        a1 = jnp.dot(xb, w1_ref[...], preferred_element_type=F32)
        a3 = jnp.dot(xb, w3_ref[...], preferred_element_type=F32)
        hid = (a1 * jax.nn.sigmoid(a1) * a3).astype(BF16)
        y = jnp.dot(hid, w2_ref[...], preferred_element_type=F32)

        @pl.when(s >= 1)
        def _():
            scatter_wait()

        ybuf[...] = y
        scatter_start(islot)

        @pl.when(s == n_act - 1)
        def _():
            scatter_wait()


def _expert_ffn(cfg, blk_e, n_act, table, h2, w1, w3, w2):
    t, d, f, bm = cfg.t, cfg.d, cfg.d_expert, EXPERT_ROWS
    nb = table.shape[0]
    wspec = lambda shape: pl.BlockSpec((None,) + shape, lambda s, e, n: (e[s], 0, 0))
    grid_spec = pltpu.PrefetchScalarGridSpec(
        num_scalar_prefetch=2,
        grid=(nb,),
        in_specs=[
            pl.BlockSpec(memory_space=pl.ANY),
            pl.BlockSpec(memory_space=pl.ANY),
            wspec((d, f)), wspec((d, f)), wspec((f, d)),
        ],
        out_specs=pl.BlockSpec(memory_space=pl.ANY),
        scratch_shapes=[
            pltpu.SMEM((3, bm), jnp.int32),
            pltpu.VMEM((2, bm, d), F32),
            pltpu.VMEM((bm, d), F32),
            pltpu.SemaphoreType.DMA((3,)),
            pltpu.SemaphoreType.DMA((2,)),
            pltpu.SemaphoreType.DMA((2,)),
        ],
    )
    vmem = 2 * 3 * d * f * 2 + 3 * bm * d * 4 + 3 * bm * d * 4
    return pl.pallas_call(
        functools.partial(_expert_kernel, n_tok=t, rows=bm),
        grid_spec=grid_spec,
        out_shape=jax.ShapeDtypeStruct((2 * t + bm, d), F32),
        compiler_params=_params(("arbitrary",), vmem),
        name="expert_ffn",
    )(blk_e, n_act, table, h2, w1, w3, w2)


def _combine_kernel(x_ref, y0_ref, y1_ref, route_ref, gate_ref, g_ref, o_ref, *, tiles_per_batch):
    b = pl.program_id(0) // tiles_per_batch
    route = route_ref[...]
    moe = route[:, 2:3] * y0_ref[...] + route[:, 3:4] * y1_ref[...]
    x2 = x_ref[...] + gate_ref[pl.ds(b, 1), :] * moe
    o_ref[...] = _rms(x2, g_ref[...])


def _combine(cfg, x1, yt, route, mod, g_final):
    t, d = cfg.t, cfg.d
    tr = _tile(cfg.seq, 256, 8)
    nt = t // tr
    gate_blk = 5
    vmem = 2 * (4 * tr * d * 4 + tr * LANES * 4 + MOD_ROWS * d * 4) + 2 * tr * d * 4
    return pl.pallas_call(
        functools.partial(_combine_kernel, tiles_per_batch=cfg.seq // tr),
        grid=(nt,),
        in_specs=[
            pl.BlockSpec((tr, d), lambda i: (i, 0)),
            pl.BlockSpec((tr, d), lambda i: (i, 0)),
            pl.BlockSpec((tr, d), lambda i: (nt + i, 0)),
            pl.BlockSpec((tr, LANES), lambda i: (i, 0)),
            pl.BlockSpec((MOD_ROWS, d), lambda i: (0, gate_blk)),
            pl.BlockSpec((1, d), lambda i: (0, 0)),
        ],
        out_specs=pl.BlockSpec((tr, d), lambda i: (i, 0)),
        out_shape=jax.ShapeDtypeStruct((t, d), F32),
        compiler_params=_params(("arbitrary",), vmem),
        name="moe_combine",
    )(x1, yt, yt, route, mod, g_final)


def _forward(cfg, x, c, ctx, c_ctx, w_mod, b_mod, g_attn, w_in, g_q, w_uq, g_kv, w_ukv,
             lam_q1, lam_k1, lam_q2, lam_k2, g_subln, w_br_a, w_br_b, w_out,
             g_ffn, w_grp, b_grp, w_exp, b_exp, w1, w3, w2, g_final):
    assert w_mod.shape[0] == 1, "single-layer block"
    b, n, d, hm = cfg.batch, cfg.seq, cfg.d, cfg.mla_heads
    off = cfg.off
    assert b + 1 <= MOD_ROWS and off["ga"] % d == 0 and off["end"] % LANES == 0
    assert cfg.epg & (cfg.epg - 1) == 0, "experts per group must be a power of two"

    cc = jnp.concatenate([c, c_ctx[None, :], jnp.zeros((MOD_ROWS - b - 1, d), F32)], axis=0)
    lamv = jnp.concatenate([lam_q1, lam_k1, lam_q2, lam_k2], axis=0)
    used = off["ckv"] + cfg.kv_lora + ROPE
    w_in_p = jnp.concatenate([w_in[0, :, :used], jnp.zeros((d, off["ga"] - used), F32), w_in[0, :, used:]],
                             axis=1).astype(BF16)
    wuq_p = jnp.pad(w_uq[0].reshape(cfg.q_lora, hm, HEAD + ROPE), ((0, 0), (0, 0), (0, HEAD - ROPE)))
    wuq_p = wuq_p.reshape(cfg.q_lora, hm * 2 * HEAD).astype(BF16)
    w_r = jnp.concatenate([w_grp[0], w_exp[0], jnp.zeros((d, LANES - cfg.n_groups - cfg.n_experts), F32)], axis=1)
    b_r = jnp.concatenate([b_grp[0], b_exp[0], jnp.zeros((LANES - cfg.n_groups - cfg.n_experts,), F32)])[None, :]
    tr = _tile(math.gcd(cfg.seq, cfg.ctx), 256, 8)
    tab = _rope_tables(cfg, tr)

    x2 = x.reshape(b * n, d)
    ctx2 = ctx.reshape(b * cfg.ctx, d)

    mod, lam = _modulation(cfg, cc, w_mod[0], b_mod, lamv)
    h = _norm_modulate(cfg, x2, ctx2, g_attn, mod)
    proj = _matmul(h, w_in_p, BF16, "in_projection")
    qd, kd, qm, km, vm = _features(cfg, proj, tab, g_q, g_kv, wuq_p, w_ukv[0].astype(BF16))
    o_a = _diff_attention(cfg, qd, kd, proj, lam, g_subln)
    o_b = _mla_attention(cfg, qm, km, vm)
    merged = _merge_branches(cfg, o_a, o_b, w_br_a[0].astype(BF16), w_br_b[0].astype(BF16), proj)
    x1 = _out_projection(cfg, merged, w_out[0].astype(BF16), x2, mod)

    h2, route = _moe_router(cfg, x1, g_ffn, mod, w_r, b_r)
    blk_e, n_act, table = _dispatch_plan(cfg, route)
    yt = _expert_ffn(cfg, blk_e, n_act, table, h2, w1[0].astype(BF16), w3[0].astype(BF16), w2[0].astype(BF16))
    out = _combine(cfg, x1, yt, route, mod, g_final[None, :])
    return out.reshape(b, n, d)


def kernel(x, c, ctx, c_ctx, w_mod, b_mod, g_attn, w_in, g_q, w_uq, g_kv, w_ukv, lam_q1, lam_k1, lam_q2, lam_k2,
           g_subln, w_br_a, w_br_b, w_out, g_ffn, w_grp, b_grp, w_exp, b_exp, w1, w3, w2, g_final):
    b, n, d = x.shape
    cfg = Cfg(batch=b, seq=n, ctx=ctx.shape[1], d=d, grid_w=64,
              da_heads=w_br_a.shape[1] // (2 * HEAD), mla_heads=w_br_b.shape[1] // HEAD,
              q_lora=w_uq.shape[1], kv_lora=w_ukv.shape[1],
              n_groups=w_grp.shape[2], epg=w_exp.shape[2] // w_grp.shape[2], d_expert=w1.shape[3],
              lambda_init=0.8 - 0.6 * math.exp(0.0))
    return _forward(cfg, x, c, ctx, c_ctx, w_mod, b_mod, g_attn, w_in, g_q, w_uq, g_kv, w_ukv,
                    lam_q1, lam_k1, lam_q2, lam_k2, g_subln, w_br_a, w_br_b, w_out,
                    g_ffn, w_grp, b_grp, w_exp, b_exp, w1, w3, w2, g_final)
```
